```python
import math
import jax, jax.numpy as jnp
from jax import lax
import numpy as np

D_MODEL = 2048
BATCH = 4
SEQ = 4096
DEPTH = 4

D_FF = 5632
DIFF_HEADS = 4
DIFF_HEAD_DIM = 64
MLA_HEADS = 6
MLA_Q_LORA = 512
MLA_KV_LORA = 512
MLA_NOPE = 128
MLA_ROPE = 64
MLA_V = 128
DIL_HEADS = 6
DIL_HEAD_DIM = 128
DILATED_PATTERNS = ((128, 1), (512, 4), (2048, 16))
DIL_BLOCK = 128
MEM_LEN = 256
CROSS_HEADS = 4
CROSS_HEAD_DIM = 128
Q_BLOCK = 128
ROPE_THETA = 10000.0
MAX_START = 1024
NEG = -1e30

A_COLS = 4 * DIFF_HEADS * DIFF_HEAD_DIM + DIFF_HEADS * 2 * DIFF_HEAD_DIM
B_COLS = MLA_Q_LORA + MLA_KV_LORA + MLA_ROPE
C_COLS = 3 * DIL_HEADS * DIL_HEAD_DIM
N_IN = A_COLS + B_COLS + C_COLS
MIX_OUT = DIFF_HEADS * 2 * DIFF_HEAD_DIM + MLA_HEADS * MLA_V + DIL_HEADS * DIL_HEAD_DIM

kernel_name = 'hybrid_parallel_headgroup_decoder'


def rmsnorm(x, g, eps=1e-6):
    xf = x.astype(jnp.float32)
    y = xf * lax.rsqrt(jnp.mean(xf * xf, axis=-1, keepdims=True) + eps)
    return (y * g.astype(jnp.float32)).astype(x.dtype)


def swiglu(x, w_gate, w_up, w_down):
    return (jax.nn.silu(x @ w_gate) * (x @ w_up)) @ w_down


def rope_tables(positions, dim):
    inv = 1.0 / (ROPE_THETA ** (jnp.arange(0, dim, 2, dtype=jnp.float32) / dim))
    ang = positions.astype(jnp.float32)[..., None] * inv
    return jnp.cos(ang), jnp.sin(ang)


def apply_rope(x, cos, sin):
    x1, x2 = jnp.split(x, 2, axis=-1)
    c = cos[:, :, None, :].astype(x.dtype)
    s = sin[:, :, None, :].astype(x.dtype)
    return jnp.concatenate([x1 * c - x2 * s, x1 * s + x2 * c], axis=-1)


def heads_first(a):
    return jnp.swapaxes(a, 1, 2)


def heads_last(a):
    B, H, S, D = a.shape
    return jnp.swapaxes(a, 1, 2).reshape(B, S, H * D)


def causal_attention(q, k, v, scale):
    B, H, S, Dk = q.shape
    Dv = v.shape[-1]
    nb = S // Q_BLOCK
    qb = q.reshape(B, H, nb, Q_BLOCK, Dk).transpose(2, 0, 1, 3, 4)
    kpos = jnp.arange(S)

    def one_block(args):
        qi, i = args
        s = jnp.einsum('bhqd,bhkd->bhqk', qi, k).astype(jnp.float32) * scale
        qpos = i * Q_BLOCK + jnp.arange(Q_BLOCK)
        s = jnp.where(kpos[None, :] <= qpos[:, None], s, NEG)
        p = jax.nn.softmax(s, axis=-1)
        return jnp.einsum('bhqk,bhkd->bhqd', p.astype(v.dtype), v)

    out = lax.map(one_block, (qb, jnp.arange(nb)))
    return out.transpose(1, 2, 0, 3, 4).reshape(B, H, S, Dv)


def dilated_window_attention(q, k, v, window, dilation, scale):
    B, H, S, D = q.shape
    Lb = DIL_BLOCK
    span = window // dilation
    unit = dilation * Lb
    Sp = -(-S // unit) * unit
    M = Sp // dilation
    nb = M // Lb

    def to_blocks(a):
        a = jnp.pad(a, ((0, 0), (0, 0), (0, Sp - S), (0, 0))).reshape(B, H, M, dilation, D)
        return jnp.swapaxes(a, 2, 3).reshape(B, H, dilation, nb, Lb, D)

    def with_prev(a):
        prev = jnp.pad(a, ((0, 0), (0, 0), (0, 0), (1, 0), (0, 0), (0, 0)))[:, :, :, :-1]
        return jnp.concatenate([prev, a], axis=4)

    qb = to_blocks(q)
    kk = with_prev(to_blocks(k))
    vv = with_prev(to_blocks(v))
    s = jnp.einsum('bhrnqd,bhrnkd->bhrnqk', qb, kk).astype(jnp.float32) * scale
    qi = jnp.arange(Lb)[:, None]
    kj = jnp.arange(2 * Lb)[None, :]
    dist = qi + Lb - kj
    blk = jnp.arange(nb)[:, None, None]
    valid = (dist >= 0) & (dist <= span) & (blk * Lb + kj - Lb >= 0)
    s = jnp.where(valid, s, NEG)
    m = jnp.max(s, axis=-1, keepdims=True)
    e = jnp.exp(s - m)
    l = jnp.sum(e, axis=-1, keepdims=True)
    o = jnp.einsum('bhrnqk,bhrnkd->bhrnqd', (e / l).astype(v.dtype), vv)
    lse = (m + jnp.log(l))[..., 0]

    def from_blocks(a):
        tail = a.shape[5:]
        a = a.reshape((B, H, dilation, M) + tail)
        a = jnp.swapaxes(a, 2, 3).reshape((B, H, Sp) + tail)
        return a[:, :, :S]

    return from_blocks(o), from_blocks(lse)


def diff_attention(u, lam_params, subln, lam_init, cos, sin):
    B, S, _ = u.shape
    n = DIFF_HEADS * DIFF_HEAD_DIM
    q1, q2, k1, k2, v = jnp.split(u, [n, 2 * n, 3 * n, 4 * n], axis=-1)
    shp = (B, S, DIFF_HEADS, DIFF_HEAD_DIM)
    q1, q2, k1, k2 = [heads_first(apply_rope(t.reshape(shp), cos, sin)) for t in (q1, q2, k1, k2)]
    vh = heads_first(v.reshape(B, S, DIFF_HEADS, 2 * DIFF_HEAD_DIM))
    scale = DIFF_HEAD_DIM ** -0.5
    a1 = causal_attention(q1, k1, vh, scale)
    a2 = causal_attention(q2, k2, vh, scale)
    lp = lam_params.astype(jnp.float32)
    lam = jnp.exp(jnp.sum(lp[0] * lp[1])) - jnp.exp(jnp.sum(lp[2] * lp[3])) + lam_init
    o = jnp.swapaxes(a1 - lam.astype(a1.dtype) * a2, 1, 2)
    o = rmsnorm(o, subln, eps=1e-5) * (1.0 - lam_init)
    return o.reshape(B, S, -1)


def mla_attention(u, q_norm, w_uq, kv_norm, w_ukv, cos, sin):
    B, S, _ = u.shape
    c_q, c_kv, k_rope = jnp.split(u, [MLA_Q_LORA, MLA_Q_LORA + MLA_KV_LORA], axis=-1)
    q = (rmsnorm(c_q, q_norm) @ w_uq).reshape(B, S, MLA_HEADS, MLA_NOPE + MLA_ROPE)
    q_nope, q_rope = jnp.split(q, [MLA_NOPE], axis=-1)
    kv = (rmsnorm(c_kv, kv_norm) @ w_ukv).reshape(B, S, MLA_HEADS, MLA_NOPE + MLA_V)
    k_nope, v = jnp.split(kv, [MLA_NOPE], axis=-1)
    q_rope = apply_rope(q_rope, cos, sin)
    k_rope = apply_rope(k_rope[:, :, None, :], cos, sin)
    q = jnp.concatenate([q_nope, q_rope], axis=-1)
    k = jnp.concatenate([k_nope, jnp.broadcast_to(k_rope, (B, S, MLA_HEADS, MLA_ROPE))], axis=-1)
    o = causal_attention(heads_first(q), heads_first(k), heads_first(v), (MLA_NOPE + MLA_ROPE) ** -0.5)
    return heads_last(o)


def dilated_attention(u, cos, sin):
    B, S, _ = u.shape
    shp = (B, S, DIL_HEADS, DIL_HEAD_DIM)
    q, k, v = [t.reshape(shp) for t in jnp.split(u, 3, axis=-1)]
    q = heads_first(apply_rope(q, cos, sin))
    k = heads_first(apply_rope(k, cos, sin))
    v = heads_first(v)
    outs, lses = [], []
    for window, dilation in DILATED_PATTERNS:
        o, lse = dilated_window_attention(q, k, v, window, dilation, DIL_HEAD_DIM ** -0.5)
        outs.append(o)
        lses.append(lse)
    wts = jax.nn.softmax(jnp.stack(lses, axis=0), axis=0)
    o = jnp.einsum('gbhs,gbhsd->bhsd', wts.astype(v.dtype), jnp.stack(outs, axis=0))
    return heads_last(o)


def cross_attention(h, memn, wq, wkv, wo):
    B, S, _ = h.shape
    Mm = memn.shape[1]
    q = (h @ wq).reshape(B, S, CROSS_HEADS, CROSS_HEAD_DIM)
    kv = (memn @ wkv).reshape(B, Mm, 2, CROSS_HEADS, CROSS_HEAD_DIM)
    k, v = kv[:, :, 0], kv[:, :, 1]
    s = jnp.einsum('bshd,bmhd->bhsm', q, k).astype(jnp.float32) * (CROSS_HEAD_DIM ** -0.5)
    p = jax.nn.softmax(s, axis=-1)
    o = jnp.einsum('bhsm,bmhd->bshd', p.astype(v.dtype), v).reshape(B, S, CROSS_HEADS * CROSS_HEAD_DIM)
    return o @ wo


def setup_inputs(seed: int = 0) -> dict:
    key = jax.random.key(seed)
    ks = jax.random.split(key, 26)
    L, D, F = DEPTH, D_MODEL, D_FF
    f32 = jnp.float32

    def w(k, shape, fan_in):
        return jax.random.normal(k, shape, f32) * (fan_in ** -0.5)

    def gain(k, shape):
        return 1.0 + 0.02 * jax.random.normal(k, shape, f32)

    x = jax.random.normal(ks[0], (BATCH, SEQ, D), f32)
    mem = jax.random.normal(ks[1], (BATCH, MEM_LEN, D), f32)
    start = jax.random.randint(ks[2], (BATCH, 1), 0, MAX_START, dtype=jnp.int32)
    positions = start + jnp.arange(SEQ, dtype=jnp.int32)[None, :]
    return {
        'x': x,
        'mem': mem,
        'positions': positions,
        'ffn1_norm': gain(ks[3], (L, D)),
        'ffn1_gate': w(ks[4], (L, D, F), D),
        'ffn1_up': w(ks[5], (L, D, F), D),
        'ffn1_down': w(ks[6], (L, F, D), F),
        'mix_norm': gain(ks[7], (L, D)),
        'w_in': w(ks[8], (L, D, N_IN), D),
        'diff_lambda': 0.1 * jax.random.normal(ks[9], (L, 4, DIFF_HEAD_DIM), f32),
        'diff_subln': gain(ks[10], (L, 2 * DIFF_HEAD_DIM)),
        'mla_q_norm': gain(ks[11], (L, MLA_Q_LORA)),
        'mla_w_uq': w(ks[12], (L, MLA_Q_LORA, MLA_HEADS * (MLA_NOPE + MLA_ROPE)), MLA_Q_LORA),
        'mla_kv_norm': gain(ks[13], (L, MLA_KV_LORA)),
        'mla_w_ukv': w(ks[14], (L, MLA_KV_LORA, MLA_HEADS * (MLA_NOPE + MLA_V)), MLA_KV_LORA),
        'w_out': w(ks[15], (L, MIX_OUT, D), MIX_OUT),
        'cross_norm': gain(ks[16], (L, D)),
        'mem_norm': gain(ks[17], (L, D)),
        'cross_wq': w(ks[18], (L, D, CROSS_HEADS * CROSS_HEAD_DIM), D),
        'cross_wkv': w(ks[19], (L, D, 2 * CROSS_HEADS * CROSS_HEAD_DIM), D),
        'cross_wo': w(ks[20], (L, CROSS_HEADS * CROSS_HEAD_DIM, D), CROSS_HEADS * CROSS_HEAD_DIM),
        'ffn2_norm': gain(ks[21], (L, D)),
        'ffn2_gate': w(ks[22], (L, D, F), D),
        'ffn2_up': w(ks[23], (L, D, F), D),
        'ffn2_down': w(ks[24], (L, F, D), F),
        'final_norm': gain(ks[25], (D,)),
    }


def reference(x, mem, positions, ffn1_norm, ffn1_gate, ffn1_up, ffn1_down, mix_norm, w_in,
              diff_lambda, diff_subln, mla_q_norm, mla_w_uq, mla_kv_norm, mla_w_ukv, w_out,
              cross_norm, mem_norm, cross_wq, cross_wkv, cross_wo,
              ffn2_norm, ffn2_gate, ffn2_up, ffn2_down, final_norm):
    cos64, sin64 = rope_tables(positions, DIFF_HEAD_DIM)
    cos128, sin128 = rope_tables(positions, DIL_HEAD_DIM)
    h = x
    for l in range(DEPTH):
        lam_init = 0.8 - 0.6 * math.exp(-0.3 * l)
        h = h + 0.5 * swiglu(rmsnorm(h, ffn1_norm[l]), ffn1_gate[l], ffn1_up[l], ffn1_down[l])
        u = rmsnorm(h, mix_norm[l]) @ w_in[l]
        u_a, u_b, u_c = jnp.split(u, [A_COLS, A_COLS + B_COLS], axis=-1)
        y_a = diff_attention(u_a, diff_lambda[l], diff_subln[l], lam_init, cos64, sin64)
        y_b = mla_attention(u_b, mla_q_norm[l], mla_w_uq[l], mla_kv_norm[l], mla_w_ukv[l], cos64, sin64)
        y_c = dilated_attention(u_c, cos128, sin128)
        h = h + jnp.concatenate([y_a, y_b, y_c], axis=-1) @ w_out[l]
        h = h + cross_attention(rmsnorm(h, cross_norm[l]), rmsnorm(mem, mem_norm[l]),
                                cross_wq[l], cross_wkv[l], cross_wo[l])
        h = h + 0.5 * swiglu(rmsnorm(h, ffn2_norm[l]), ffn2_gate[l], ffn2_up[l], ffn2_down[l])
    return rmsnorm(h, final_norm)
```

```python
import functools
import math

import jax
import jax.numpy as jnp
from jax import lax
from jax.experimental import pallas as pl
from jax.experimental.pallas import tpu as pltpu

D_MODEL = 2048
BATCH = 4
SEQ = 4096
DEPTH = 4
D_FF = 5632
DIFF_HEADS = 4
DIFF_HEAD_DIM = 64
MLA_HEADS = 6
MLA_Q_LORA = 512
MLA_KV_LORA = 512
MLA_NOPE = 128
MLA_ROPE = 64
MLA_V = 128
DIL_HEADS = 6
DIL_HEAD_DIM = 128
DILATIONS = (1, 4, 16)
DIL_BLOCK = 128
MEM_LEN = 256
CROSS_HEADS = 4
CROSS_HEAD_DIM = 128
ROPE_THETA = 10000.0
NEG = -1e30

A_COLS = 4 * DIFF_HEADS * DIFF_HEAD_DIM + DIFF_HEADS * 2 * DIFF_HEAD_DIM
B_COLS = MLA_Q_LORA + MLA_KV_LORA + MLA_ROPE
C_COLS = 3 * DIL_HEADS * DIL_HEAD_DIM

LANES = 128
VMEM_LIMIT = 56 * 1024 * 1024

BF16 = jnp.bfloat16
F32 = jnp.float32


def _cparams(sem):
    return pltpu.CompilerParams(dimension_semantics=sem, vmem_limit_bytes=VMEM_LIMIT)


def _rms_rows(x, g, eps):
    ms = jnp.mean(x * x, axis=-1, keepdims=True)
    return x * lax.rsqrt(ms + eps) * g


def _dot(a, b):
    return jnp.dot(a, b, preferred_element_type=F32)


def _dot_nt(a, b):
    return lax.dot_general(a, b, (((1,), (1,)), ((), ())), preferred_element_type=F32)


def _swap_halves(a, half):
    if 2 * half == LANES:
        return pltpu.roll(a, half, axis=1)
    lane = lax.broadcasted_iota(jnp.int32, a.shape, 1)
    first = (lane % (2 * half)) < half
    return jnp.where(first, pltpu.roll(a, LANES - half, axis=1), pltpu.roll(a, half, axis=1))


def _norm_matmul_kernel(*refs, eps, rope_half, has_scale):
    x_ref, g_ref, w_ref = refs[:3]
    pos = 3
    if rope_half:
        cos_ref, sin_ref = refs[pos:pos + 2]
        pos += 2
    if has_scale:
        cs_ref = refs[pos]
        pos += 1
    o_ref, xn_ref = refs[pos:pos + 2]

    @pl.when(pl.program_id(1) == 0)
    def _():
        xn_ref[...] = _rms_rows(x_ref[...].astype(F32), g_ref[...], eps).astype(BF16)

    acc = _dot(xn_ref[...], w_ref[...])
    tn = acc.shape[1]
    if rope_half:
        cos = cos_ref[...]
        sin = sin_ref[...]
        parts = []
        for c in range(tn // LANES):
            a = acc[:, c * LANES:(c + 1) * LANES]
            parts.append(a * cos + _swap_halves(a, rope_half) * sin)
        acc = parts[0] if len(parts) == 1 else jnp.concatenate(parts, axis=1)
    if has_scale:
        acc = acc * cs_ref[...]
    o_ref[...] = acc.astype(o_ref.dtype)


def norm_matmul(x, g, w, *, eps=1e-6, x_col_block=0, tm, tn, out_dtype,
                rope_half=0, cos=None, sin=None, colscale=None, name):
    t = x.shape[0]
    dk, n = w.shape
    assert t % tm == 0 and n % tn == 0 and tn % LANES == 0
    in_specs = [
        pl.BlockSpec((tm, dk), lambda i, j: (i, x_col_block)),
        pl.BlockSpec((1, dk), lambda i, j: (0, 0)),
        pl.BlockSpec((dk, tn), lambda i, j: (0, j)),
    ]
    args = [x, g.reshape(1, dk).astype(F32), w]
    if rope_half:
        in_specs += [pl.BlockSpec((tm, LANES), lambda i, j: (i, 0))] * 2
        args += [cos, sin]
    if colscale is not None:
        in_specs.append(pl.BlockSpec((1, tn), lambda i, j: (0, j)))
        args.append(colscale)
    return pl.pallas_call(
        functools.partial(_norm_matmul_kernel, eps=eps, rope_half=rope_half,
                          has_scale=colscale is not None),
        grid=(t // tm, n // tn),
        in_specs=in_specs,
        out_specs=pl.BlockSpec((tm, tn), lambda i, j: (i, j)),
        out_shape=jax.ShapeDtypeStruct((t, n), out_dtype),
        scratch_shapes=[pltpu.VMEM((tm, dk), BF16)],
        compiler_params=_cparams(("parallel", "arbitrary")),
        name=name,
    )(*args)


def _ffn_up_kernel(x_ref, g_ref, wg_ref, wu_ref, o_ref, xn_ref):
    @pl.when(pl.program_id(1) == 0)
    def _():
        xn_ref[...] = _rms_rows(x_ref[...], g_ref[...], 1e-6).astype(BF16)

    xn = xn_ref[...]
    gate = _dot(xn, wg_ref[...])
    up = _dot(xn, wu_ref[...])
    o_ref[...] = (gate * jax.nn.sigmoid(gate) * up).astype(o_ref.dtype)


def ffn_up(h, g, wg, wu, *, tm=512, tn=512):
    t, d = h.shape
    f = wg.shape[1]
    return pl.pallas_call(
        _ffn_up_kernel,
        grid=(t // tm, f // tn),
        in_specs=[
            pl.BlockSpec((tm, d), lambda i, j: (i, 0)),
            pl.BlockSpec((1, d), lambda i, j: (0, 0)),
            pl.BlockSpec((d, tn), lambda i, j: (0, j)),
            pl.BlockSpec((d, tn), lambda i, j: (0, j)),
        ],
        out_specs=pl.BlockSpec((tm, tn), lambda i, j: (i, j)),
        out_shape=jax.ShapeDtypeStruct((t, f), BF16),
        scratch_shapes=[pltpu.VMEM((tm, d), BF16)],
        compiler_params=_cparams(("parallel", "arbitrary")),
        name="ffn_up",
    )(h, g.reshape(1, d), wg, wu)


def _matmul_res_kernel(*refs, n_x, res_scale):
    xs = [r[...] for r in refs[:n_x]]
    w_ref, res_ref, o_ref = refs[n_x:n_x + 3]
    x = xs[0] if n_x == 1 else jnp.concatenate(xs, axis=1)
    acc = _dot(x, w_ref[...])
    if res_scale != 1.0:
        acc = acc * res_scale
    o_ref[...] = res_ref[...] + acc


def matmul_res(xs, w, res, *, res_scale=1.0, tm=512, tn=512, name):
    t = res.shape[0]
    k, n = w.shape
    assert sum(x.shape[1] for x in xs) == k
    in_specs = [pl.BlockSpec((tm, x.shape[1]), lambda i, j: (i, 0)) for x in xs]
    in_specs += [
        pl.BlockSpec((k, tn), lambda i, j: (0, j)),
        pl.BlockSpec((tm, tn), lambda i, j: (i, j)),
    ]
    return pl.pallas_call(
        functools.partial(_matmul_res_kernel, n_x=len(xs), res_scale=res_scale),
        grid=(t // tm, n // tn),
        in_specs=in_specs,
        out_specs=pl.BlockSpec((tm, tn), lambda i, j: (i, j)),
        out_shape=jax.ShapeDtypeStruct((t, n), F32),
        compiler_params=_cparams(("parallel", "parallel")),
        name=name,
    )(*xs, w, res)


def _online_softmax_update(s, m_ref, l_ref):
    m_old = m_ref[...]
    m_new = jnp.maximum(m_old, jnp.max(s, axis=-1, keepdims=True))
    alpha = jnp.exp(m_old - m_new)
    p = jnp.exp(s - m_new)
    l_ref[...] = alpha * l_ref[...] + jnp.sum(p, axis=-1, keepdims=True)
    m_ref[...] = m_new
    return p, alpha


def _diff_attn_kernel(lam_ref, sub_ref, q1_ref, q2_ref, k1_ref, k2_ref, v_ref, o_ref,
                      qs1, qs2, m1, l1, acc1, m2, l2, acc2, *, tq, lam_init):
    qi = pl.program_id(2)
    hd = DIFF_HEAD_DIM
    lane = lax.broadcasted_iota(jnp.int32, (tq, LANES), 1)
    first = lane < hd

    def stack(q_ref, qs):
        q = q_ref[0].astype(F32)
        qs[0:tq, :] = jnp.where(first, q, 0.0).astype(BF16)
        qs[tq:2 * tq, :] = jnp.where(first, 0.0, q).astype(BF16)

    stack(q1_ref, qs1)
    stack(q2_ref, qs2)
    for m, l, acc in ((m1, l1, acc1), (m2, l2, acc2)):
        m[...] = jnp.full(m.shape, NEG, F32)
        l[...] = jnp.zeros(l.shape, F32)
        acc[...] = jnp.zeros(acc.shape, F32)

    row = lax.broadcasted_iota(jnp.int32, (2 * tq, tq), 0) % tq
    col = lax.broadcasted_iota(jnp.int32, (2 * tq, tq), 1)
    causal = col <= row

    def block(ki, masked):
        start = pl.multiple_of(ki * tq, tq)
        v = v_ref[0, pl.ds(start, tq), :].astype(BF16)
        for qs, k_ref, m, l, acc in ((qs1, k1_ref, m1, l1, acc1), (qs2, k2_ref, m2, l2, acc2)):
            k = k_ref[0, pl.ds(start, tq), :]
            s = _dot_nt(qs[...], k)
            if masked:
                s = jnp.where(causal, s, NEG)
            p, alpha = _online_softmax_update(s, m, l)
            p = p.astype(BF16)
            pv = jnp.concatenate([_dot(p[0:tq], v[:, 0:LANES]),
                                  _dot(p[tq:2 * tq], v[:, LANES:2 * LANES])], axis=0)
            acc[...] = alpha * acc[...] + pv

    def body(ki, carry):
        block(ki, False)
        return carry

    lax.fori_loop(0, qi, body, 0)
    block(qi, True)

    lp = lam_ref[...]
    lam = (jnp.exp(jnp.sum(lp[0:1] * lp[1:2], keepdims=True))
           - jnp.exp(jnp.sum(lp[2:3] * lp[3:4], keepdims=True)) + lam_init)
    d = acc1[...] / l1[...] - lam * (acc2[...] / l2[...])
    y = _rms_rows(d, sub_ref[...], 1e-5) * (1.0 - lam_init)
    o_ref[0, :, 0:LANES] = y[0:tq].astype(o_ref.dtype)
    o_ref[0, :, LANES:2 * LANES] = y[tq:2 * tq].astype(o_ref.dtype)


def diff_attention(u64, u_plain, lam_params, subln, lam_init, *, tq=256):
    b, s, _ = u64.shape
    pairs = DIFF_HEADS // 2
    qspec = lambda blk0: pl.BlockSpec((1, tq, LANES), lambda bi, p, qi: (bi, qi, blk0 + p))
    kspec = lambda blk0: pl.BlockSpec((1, s, LANES), lambda bi, p, qi: (bi, 0, blk0 + p))
    return pl.pallas_call(
        functools.partial(_diff_attn_kernel, tq=tq, lam_init=lam_init),
        grid=(b, pairs, s // tq),
        in_specs=[
            pl.BlockSpec((4, DIFF_HEAD_DIM), lambda bi, p, qi: (0, 0)),
            pl.BlockSpec((1, 2 * DIFF_HEAD_DIM), lambda bi, p, qi: (0, 0)),
            qspec(0), qspec(pairs), kspec(2 * pairs), kspec(3 * pairs),
            pl.BlockSpec((1, s, 2 * LANES), lambda bi, p, qi: (bi, 0, p)),
        ],
        out_specs=pl.BlockSpec((1, tq, 2 * LANES), lambda bi, p, qi: (bi, qi, p)),
        out_shape=jax.ShapeDtypeStruct((b, s, DIFF_HEADS * 2 * DIFF_HEAD_DIM), BF16),
        scratch_shapes=[
            pltpu.VMEM((2 * tq, LANES), BF16), pltpu.VMEM((2 * tq, LANES), BF16),
            pltpu.VMEM((2 * tq, 1), F32), pltpu.VMEM((2 * tq, 1), F32), pltpu.VMEM((2 * tq, LANES), F32),
            pltpu.VMEM((2 * tq, 1), F32), pltpu.VMEM((2 * tq, 1), F32), pltpu.VMEM((2 * tq, LANES), F32),
        ],
        compiler_params=_cparams(("parallel", "parallel", "arbitrary")),
        name="diff_attn",
    )(lam_params, subln.reshape(1, -1), u64, u64, u64, u64, u_plain)


def _mla_attn_kernel(qn_ref, qr_ref, kv_ref, kr_ref, o_ref, qs, m, l, acc, *, tq):
    h = pl.program_id(1)
    qi = pl.program_id(2)
    lane = lax.broadcasted_iota(jnp.int32, (tq, LANES), 1)
    mine = (lane // MLA_ROPE) == (h % 2)
    qs[:, 0:LANES] = qn_ref[0]
    qs[:, LANES:2 * LANES] = jnp.where(mine, qr_ref[0].astype(F32), 0.0).astype(BF16)
    m[...] = jnp.full(m.shape, NEG, F32)
    l[...] = jnp.zeros(l.shape, F32)
    acc[...] = jnp.zeros(acc.shape, F32)

    row = lax.broadcasted_iota(jnp.int32, (tq, tq), 0)
    col = lax.broadcasted_iota(jnp.int32, (tq, tq), 1)
    causal = col <= row

    def block(ki, masked):
        start = pl.multiple_of(ki * tq, tq)
        kv = kv_ref[0, pl.ds(start, tq), :]
        kr = kr_ref[0, pl.ds(start, tq), :]
        k = jnp.concatenate([kv[:, 0:LANES], kr], axis=1)
        s = _dot_nt(qs[...], k)
        if masked:
            s = jnp.where(causal, s, NEG)
        p, alpha = _online_softmax_update(s, m, l)
        acc[...] = alpha * acc[...] + _dot(p.astype(BF16), kv[:, LANES:2 * LANES])

    def body(ki, carry):
        block(ki, False)
        return carry

    lax.fori_loop(0, qi, body, 0)
    block(qi, True)
    o_ref[0] = (acc[...] / l[...]).astype(o_ref.dtype)


def mla_attention(qn, qr, kv, u64, *, tq=256):
    b, s, _ = qn.shape
    kr_block = (u64.shape[2] // LANES) - 1
    return pl.pallas_call(
        functools.partial(_mla_attn_kernel, tq=tq),
        grid=(b, MLA_HEADS, s // tq),
        in_specs=[
            pl.BlockSpec((1, tq, LANES), lambda bi, h, qi: (bi, qi, h)),
            pl.BlockSpec((1, tq, LANES), lambda bi, h, qi: (bi, qi, h // 2)),
            pl.BlockSpec((1, s, 2 * LANES), lambda bi, h, qi: (bi, 0, h)),
            pl.BlockSpec((1, s, LANES), lambda bi, h, qi: (bi, 0, kr_block)),
        ],
        out_specs=pl.BlockSpec((1, tq, LANES), lambda bi, h, qi: (bi, qi, h)),
        out_shape=jax.ShapeDtypeStruct((b, s, MLA_HEADS * MLA_V), BF16),
        scratch_shapes=[
            pltpu.VMEM((tq, 2 * LANES), BF16),
            pltpu.VMEM((tq, 1), F32), pltpu.VMEM((tq, 1), F32), pltpu.VMEM((tq, LANES), F32),
        ],
        compiler_params=_cparams(("parallel", "parallel", "arbitrary")),
        name="mla_attn",
    )(qn, qr, kv, u64)


def _dil_attn_kernel(q_ref, kc_ref, kp_ref, vc_ref, vp_ref, o_ref, lse_ref):
    n = pl.program_id(2)
    lb = DIL_BLOCK
    qi = lax.broadcasted_iota(jnp.int32, (lb, lb), 0)
    kj = lax.broadcasted_iota(jnp.int32, (lb, lb), 1)
    cur_ok = kj <= qi
    prev_ok = (kj >= qi) & (n > 0)
    for h in range(DIL_HEADS):
        sl = slice(h * LANES, (h + 1) * LANES)
        q = q_ref[0, :, sl]
        s_c = jnp.where(cur_ok, _dot_nt(q, kc_ref[0, :, sl]), NEG)
        s_p = jnp.where(prev_ok, _dot_nt(q, kp_ref[0, :, sl]), NEG)
        m = jnp.maximum(jnp.max(s_c, axis=-1, keepdims=True), jnp.max(s_p, axis=-1, keepdims=True))
        e_c = jnp.exp(s_c - m)
        e_p = jnp.exp(s_p - m)
        l = jnp.sum(e_c, axis=-1, keepdims=True) + jnp.sum(e_p, axis=-1, keepdims=True)
        o = (_dot((e_c / l).astype(BF16), vc_ref[0, :, sl].astype(BF16))
             + _dot((e_p / l).astype(BF16), vp_ref[0, :, sl].astype(BF16)))
        o_ref[0, :, sl] = o
        lse_ref[0, :, sl] = jnp.broadcast_to(m + jnp.log(l), (lb, LANES))


def dilated_pattern(u128, u_plain, d):
    b, s, wqk = u128.shape
    wp = u_plain.shape[2]
    hw = DIL_HEADS * DIL_HEAD_DIM
    lb = DIL_BLOCK
    qk = u128.reshape(b, s // d, d * wqk)
    vv = u_plain.reshape(b, s // d, d * wp)
    qk_per = wqk // hw
    v_per = wp // hw
    prev = lambda n: jnp.maximum(n - 1, 0)
    out_shape = jax.ShapeDtypeStruct((b, s // d, d * hw), F32)
    o, lse = pl.pallas_call(
        _dil_attn_kernel,
        grid=(b, d, s // d // lb),
        in_specs=[
            pl.BlockSpec((1, lb, hw), lambda bi, r, n: (bi, n, r * qk_per)),
            pl.BlockSpec((1, lb, hw), lambda bi, r, n: (bi, n, r * qk_per + 1)),
            pl.BlockSpec((1, lb, hw), lambda bi, r, n: (bi, prev(n), r * qk_per + 1)),
            pl.BlockSpec((1, lb, hw), lambda bi, r, n: (bi, n, r * v_per + v_per - 1)),
            pl.BlockSpec((1, lb, hw), lambda bi, r, n: (bi, prev(n), r * v_per + v_per - 1)),
        ],
        out_specs=[pl.BlockSpec((1, lb, hw), lambda bi, r, n: (bi, n, r))] * 2,
        out_shape=[out_shape, out_shape],
        compiler_params=_cparams(("parallel", "parallel", "parallel")),
        name=f"dil_attn_d{d}",
    )(qk, qk, qk, vv, vv)
    return o.reshape(b * s, hw), lse.reshape(b * s, hw)


def _dil_merge_kernel(o1, l1, o2, l2, o3, l3, y_ref):
    a, bb, c = l1[...], l2[...], l3[...]
    m = jnp.maximum(jnp.maximum(a, bb), c)
    ea, eb, ec = jnp.exp(a - m), jnp.exp(bb - m), jnp.exp(c - m)
    tot = ea + eb + ec
    y_ref[...] = ((ea / tot) * o1[...] + (eb / tot) * o2[...] + (ec / tot) * o3[...]).astype(y_ref.dtype)


def dilated_merge(parts, *, tm=512):
    t, hw = parts[0][0].shape
    flat = [a for pair in parts for a in pair]
    spec = pl.BlockSpec((tm, hw), lambda i: (i, 0))
    return pl.pallas_call(
        _dil_merge_kernel,
        grid=(t // tm,),
        in_specs=[spec] * 6,
        out_specs=spec,
        out_shape=jax.ShapeDtypeStruct((t, hw), BF16),
        compiler_params=_cparams(("parallel",)),
        name="dil_merge",
    )(*flat)


def _cross_kernel(h_ref, g_ref, wq_ref, kv_ref, wo_ref, o_ref):
    hd = CROSS_HEAD_DIM
    width = CROSS_HEADS * hd
    h = h_ref[...]
    xn = _rms_rows(h, g_ref[...], 1e-6).astype(BF16)
    q = (_dot(xn, wq_ref[...]) * (hd ** -0.5)).astype(BF16)
    kv = kv_ref[...]
    outs = []
    for i in range(CROSS_HEADS):
        k = kv[:, i * hd:(i + 1) * hd]
        v = kv[:, width + i * hd:width + (i + 1) * hd]
        s = _dot_nt(q[:, i * hd:(i + 1) * hd], k)
        m = jnp.max(s, axis=-1, keepdims=True)
        e = jnp.exp(s - m)
        p = e / jnp.sum(e, axis=-1, keepdims=True)
        outs.append(_dot(p.astype(BF16), v))
    o = jnp.concatenate(outs, axis=1).astype(BF16)
    o_ref[...] = h + _dot(o, wo_ref[...])


def cross_attention(h, g, wq, kv, wo, *, tq=512):
    t, d = h.shape
    width = CROSS_HEADS * CROSS_HEAD_DIM
    per_batch = SEQ // tq
    return pl.pallas_call(
        _cross_kernel,
        grid=(t // tq,),
        in_specs=[
            pl.BlockSpec((tq, d), lambda i: (i, 0)),
            pl.BlockSpec((1, d), lambda i: (0, 0)),
            pl.BlockSpec((d, width), lambda i: (0, 0)),
            pl.BlockSpec((MEM_LEN, 2 * width), lambda i: (i // per_batch, 0)),
            pl.BlockSpec((width, d), lambda i: (0, 0)),
        ],
        out_specs=pl.BlockSpec((tq, d), lambda i: (i, 0)),
        out_shape=jax.ShapeDtypeStruct((t, d), F32),
        compiler_params=_cparams(("parallel",)),
        name="cross_attn",
    )(h, g.reshape(1, d), wq, kv, wo)


def _final_norm_kernel(x_ref, g_ref, o_ref):
    o_ref[...] = _rms_rows(x_ref[...], g_ref[...], 1e-6)


def final_rmsnorm(h, g, *, tm=512):
    t, d = h.shape
    return pl.pallas_call(
        _final_norm_kernel,
        grid=(t // tm,),
        in_specs=[pl.BlockSpec((tm, d), lambda i: (i, 0)), pl.BlockSpec((1, d), lambda i: (0, 0))],
        out_specs=pl.BlockSpec((tm, d), lambda i: (i, 0)),
        out_shape=jax.ShapeDtypeStruct((t, d), F32),
        compiler_params=_cparams(("parallel",)),
        name="final_norm",
    )(h, g.reshape(1, d))


def _rope_tables(positions, dim, reps):
    inv = 1.0 / (ROPE_THETA ** (jnp.arange(0, dim, 2, dtype=F32) / dim))
    ang = positions.astype(F32).reshape(-1, 1) * inv
    cos, sin = jnp.cos(ang), jnp.sin(ang)
    cos_t = jnp.tile(jnp.concatenate([cos, cos], axis=1), (1, reps))
    sin_t = jnp.tile(jnp.concatenate([-sin, sin], axis=1), (1, reps))
    return cos_t, sin_t


def _layer_weights(l, w_in, mla_w_uq):
    w = w_in[l]
    a_rot = 4 * DIFF_HEADS * DIFF_HEAD_DIM
    b0 = A_COLS
    kr0 = b0 + MLA_Q_LORA + MLA_KV_LORA
    c0 = A_COLS + B_COLS
    c_rot = 2 * DIL_HEADS * DIL_HEAD_DIM
    k_rope = w[:, kr0:kr0 + MLA_ROPE]
    w64 = jnp.concatenate([w[:, :a_rot], k_rope, k_rope], axis=1).astype(BF16)
    w128 = w[:, c0:c0 + c_rot].astype(BF16)
    wplain = jnp.concatenate([w[:, a_rot:A_COLS], w[:, b0:kr0], w[:, c0 + c_rot:]], axis=1).astype(BF16)
    uq = mla_w_uq[l].reshape(MLA_Q_LORA, MLA_HEADS, MLA_NOPE + MLA_ROPE)
    w_qn = uq[:, :, :MLA_NOPE].reshape(MLA_Q_LORA, MLA_HEADS * MLA_NOPE).astype(BF16)
    w_qr = uq[:, :, MLA_NOPE:].reshape(MLA_Q_LORA, MLA_HEADS * MLA_ROPE).astype(BF16)
    return w64, w128, wplain, w_qn, w_qr


def kernel(x, mem, positions, ffn1_norm, ffn1_gate, ffn1_up, ffn1_down, mix_norm, w_in, diff_lambda, diff_subln, mla_q_norm, mla_w_uq, mla_kv_norm, mla_w_ukv, w_out, cross_norm, mem_norm, cross_wq, cross_wkv, cross_wo, ffn2_norm, ffn2_gate, ffn2_up, ffn2_down, final_norm):
    b, s, d = x.shape
    t = b * s
    cos64, sin64 = _rope_tables(positions, DIFF_HEAD_DIM, 2)
    cos128, sin128 = _rope_tables(positions, DIL_HEAD_DIM, 1)
    a_rot = 4 * DIFF_HEADS * DIFF_HEAD_DIM
    n64 = a_rot + 2 * MLA_ROPE
    scale64 = jnp.concatenate([jnp.full((1, a_rot // 2), DIFF_HEAD_DIM ** -0.5, F32),
                               jnp.ones((1, n64 - a_rot // 2), F32)], axis=1)
    hw = DIL_HEADS * DIL_HEAD_DIM
    scale128 = jnp.concatenate([jnp.full((1, hw), DIL_HEAD_DIM ** -0.5, F32), jnp.ones((1, hw), F32)], axis=1)
    mla_scale = (MLA_NOPE + MLA_ROPE) ** -0.5
    scale_qn = jnp.full((1, MLA_HEADS * MLA_NOPE), mla_scale, F32)
    scale_qr = jnp.full((1, MLA_HEADS * MLA_ROPE), mla_scale, F32)
    memf = mem.reshape(b * MEM_LEN, d)

    h = x.reshape(t, d)
    for l in range(DEPTH):
        lam_init = 0.8 - 0.6 * math.exp(-0.3 * l)
        w64, w128, wplain, w_qn, w_qr = _layer_weights(l, w_in, mla_w_uq)

        act = ffn_up(h, ffn1_norm[l], ffn1_gate[l].astype(BF16), ffn1_up[l].astype(BF16))
        h = matmul_res([act], ffn1_down[l].astype(BF16), h, res_scale=0.5, name="ffn_down")

        u64 = norm_matmul(h, mix_norm[l], w64, tm=512, tn=384, out_dtype=BF16, rope_half=DIFF_HEAD_DIM // 2,
                          cos=cos64, sin=sin64, colscale=scale64, name="in_proj_rope64")
        u128 = norm_matmul(h, mix_norm[l], w128, tm=512, tn=512, out_dtype=BF16, rope_half=DIL_HEAD_DIM // 2,
                           cos=cos128, sin=sin128, colscale=scale128, name="in_proj_rope128")
        u_plain = norm_matmul(h, mix_norm[l], wplain, tm=512, tn=768, out_dtype=F32, name="in_proj_plain")

        qn = norm_matmul(u_plain, mla_q_norm[l], w_qn, x_col_block=1, tm=512, tn=768, out_dtype=BF16,
                         colscale=scale_qn, name="mla_q_nope")
        qr = norm_matmul(u_plain, mla_q_norm[l], w_qr, x_col_block=1, tm=512, tn=384, out_dtype=BF16,
                         rope_half=MLA_ROPE // 2, cos=cos64, sin=sin64, colscale=scale_qr, name="mla_q_rope")
        kv = norm_matmul(u_plain, mla_kv_norm[l], mla_w_ukv[l].astype(BF16), x_col_block=2, tm=512, tn=768,
                         out_dtype=BF16, name="mla_kv")

        u64_3 = u64.reshape(b, s, -1)
        u128_3 = u128.reshape(b, s, -1)
        up_3 = u_plain.reshape(b, s, -1)
        y_a = diff_attention(u64_3, up_3, diff_lambda[l], diff_subln[l], lam_init)
        y_b = mla_attention(qn.reshape(b, s, -1), qr.reshape(b, s, -1), kv.reshape(b, s, -1), u64_3)
        y_c = dilated_merge([dilated_pattern(u128_3, up_3, dd) for dd in DILATIONS])
        h = matmul_res([y_a.reshape(t, -1), y_b.reshape(t, -1), y_c], w_out[l].astype(BF16), h, name="out_proj")

        kvm = norm_matmul(memf, mem_norm[l], cross_wkv[l].astype(BF16), tm=256, tn=512, out_dtype=BF16,
                          name="cross_kv")
        h = cross_attention(h, cross_norm[l], cross_wq[l].astype(BF16), kvm, cross_wo[l].astype(BF16))

        act = ffn_up(h, ffn2_norm[l], ffn2_gate[l].astype(BF16), ffn2_up[l].astype(BF16))
        h = matmul_res([act], ffn2_down[l].astype(BF16), h, res_scale=0.5, name="ffn_down")
    return final_rmsnorm(h, final_norm).reshape(b, s, d)
```
